```python
import jax, jax.numpy as jnp
from jax import lax
import numpy as np

D_MODEL = 1024
BATCH = 8
SEQ = 8192
DEPTH = 2

CHUNK = 64
BAND_CHUNKS = 9
ATTN_HEADS = 8
HEAD_DIM = 64
ATTN_WIDTH = ATTN_HEADS * HEAD_DIM
POOL_WINDOWS = (2, 4, 8, 16)
POOL_GROUPS = len(POOL_WINDOWS)
POOL_WIDTH = D_MODEL // 2
POOL_GROUP_DIM = POOL_WIDTH // POOL_GROUPS
MAX_REL_DIST = 256
N_REL = 2 * MAX_REL_DIST + 1
D_FF = 2816
CONV_WIDTH = 3
N_BRANCH = 2
IN_WIDTH = 3 * ATTN_WIDTH + POOL_WIDTH + N_BRANCH * D_MODEL
EPS = 1e-6

kernel_name = "hybrid_chunk_attn_pool_sandwich"


def rms_norm(x, g):
    xf = x.astype(jnp.float32)
    y = xf * lax.rsqrt(jnp.mean(xf * xf, axis=-1, keepdims=True) + EPS)
    return (y * g.astype(jnp.float32)).astype(x.dtype)


def chunk_band_attention(q, k, v, rel_bias):
    b, s, h, dh = q.shape
    n_chunks = s // CHUNK
    band = BAND_CHUNKS * CHUNK
    lead = (BAND_CHUNKS - 1) * CHUNK
    pad = ((0, 0), (lead, 0), (0, 0), (0, 0))
    k_pad = jnp.pad(k, pad)
    v_pad = jnp.pad(v, pad)
    dist = jnp.arange(CHUNK)[:, None] + lead - jnp.arange(band)[None, :]
    idx = jnp.clip(dist, -MAX_REL_DIST, MAX_REL_DIST) + MAX_REL_DIST
    bias = rel_bias.astype(jnp.float32)[:, idx]
    scale = HEAD_DIM ** -0.5
    key_offsets = jnp.arange(band)

    def one_chunk(c):
        start = c * CHUNK
        q_c = lax.dynamic_slice_in_dim(q, start, CHUNK, axis=1)
        k_c = lax.dynamic_slice_in_dim(k_pad, start, band, axis=1)
        v_c = lax.dynamic_slice_in_dim(v_pad, start, band, axis=1)
        sc = jnp.einsum('bqhd,bkhd->bhqk', q_c, k_c,
                        preferred_element_type=jnp.float32) * scale + bias
        valid = (start - lead + key_offsets) >= 0
        sc = jnp.where(valid[None, None, None, :], sc, -1e30)
        p = jax.nn.softmax(sc, axis=-1).astype(v.dtype)
        return jnp.einsum('bhqk,bkhd->bqhd', p, v_c)

    out = lax.map(one_chunk, jnp.arange(n_chunks))
    return jnp.moveaxis(out, 0, 1).reshape(b, s, h * dh)


def multiscale_pool(u, w_group, scale):
    b, s, c = u.shape
    uf = u.astype(jnp.float32)
    max_w = max(POOL_WINDOWS)
    cs = jnp.pad(jnp.cumsum(uf, axis=1), ((0, 0), (max_w, 0), (0, 0)))
    t = jnp.arange(s)
    outs = []
    for g, w in enumerate(POOL_WINDOWS):
        sl = slice(g * POOL_GROUP_DIM, (g + 1) * POOL_GROUP_DIM)
        win = cs[:, max_w:, sl] - cs[:, max_w - w:max_w - w + s, sl]
        cnt = jnp.minimum(t + 1, w).astype(jnp.float32)[None, :, None]
        outs.append(win / cnt - uf[:, :, sl])
    pooled = jnp.stack(outs, axis=2).astype(u.dtype)
    mixed = jnp.einsum('bsgc,gcd->bsgd', pooled, w_group).reshape(b, s, c)
    return mixed * scale


def conv_gated_ffn(x, w_up, conv_w, conv_b, w_down):
    hu = x @ w_up
    s = hu.shape[1]
    hp = jnp.pad(hu, ((0, 0), (CONV_WIDTH - 1, 0), (0, 0)))
    hc = conv_b + conv_w[CONV_WIDTH - 1] * hu
    for i in range(CONV_WIDTH - 1):
        hc = hc + conv_w[i] * hp[:, i:i + s]
    val, gate = jnp.split(hc, 2, axis=-1)
    return (jax.nn.gelu(gate, approximate=True) * val) @ w_down


def setup_inputs(seed: int = 0) -> dict:
    key = jax.random.key(seed)
    ks = jax.random.split(key, 20)
    f32 = jnp.float32

    def nrm(k, shape, s):
        return jax.random.normal(k, shape, f32) * s

    return {
        "x": jax.random.normal(ks[0], (BATCH, SEQ, D_MODEL), f32),
        "norm_mix_pre": 1.0 + nrm(ks[1], (DEPTH, D_MODEL), 0.05),
        "w_in": nrm(ks[2], (DEPTH, D_MODEL, IN_WIDTH), D_MODEL ** -0.5),
        "b_gate": nrm(ks[3], (DEPTH, N_BRANCH * D_MODEL), 0.01),
        "rel_bias": nrm(ks[4], (DEPTH, ATTN_HEADS, N_REL), 0.1),
        "w_attn_out": nrm(ks[5], (DEPTH, ATTN_WIDTH, D_MODEL), ATTN_WIDTH ** -0.5),
        "w_pool_group": nrm(ks[6], (DEPTH, POOL_GROUPS, POOL_GROUP_DIM, POOL_GROUP_DIM), POOL_GROUP_DIM ** -0.5),
        "pool_scale": 1.0 + nrm(ks[7], (DEPTH, POOL_WIDTH), 0.1),
        "w_pool_out": nrm(ks[8], (DEPTH, POOL_WIDTH, D_MODEL), POOL_WIDTH ** -0.5),
        "w_o": nrm(ks[9], (DEPTH, D_MODEL, D_MODEL), D_MODEL ** -0.5),
        "norm_mix_post": 1.0 + nrm(ks[10], (DEPTH, D_MODEL), 0.05),
        "norm_ffn_pre": 1.0 + nrm(ks[11], (DEPTH, D_MODEL), 0.05),
        "w_up": nrm(ks[12], (DEPTH, D_MODEL, 2 * D_FF), D_MODEL ** -0.5),
        "conv_w": nrm(ks[13], (DEPTH, CONV_WIDTH, 2 * D_FF), CONV_WIDTH ** -0.5),
        "conv_b": nrm(ks[14], (DEPTH, 2 * D_FF), 0.01),
        "w_down": nrm(ks[15], (DEPTH, D_FF, D_MODEL), D_FF ** -0.5),
        "norm_ffn_post": 1.0 + nrm(ks[16], (DEPTH, D_MODEL), 0.05),
    }


def reference(x, norm_mix_pre, w_in, b_gate, rel_bias, w_attn_out, w_pool_group, pool_scale,
              w_pool_out, w_o, norm_mix_post, norm_ffn_pre, w_up, conv_w, conv_b, w_down,
              norm_ffn_post):
    b, s, _ = x.shape
    splits = [ATTN_WIDTH, 2 * ATTN_WIDTH, 3 * ATTN_WIDTH, 3 * ATTN_WIDTH + POOL_WIDTH]
    for l in range(DEPTH):
        h = rms_norm(x, norm_mix_pre[l])
        proj = h @ w_in[l]
        q, k, v, u, gates = jnp.split(proj, splits, axis=-1)
        q = q.reshape(b, s, ATTN_HEADS, HEAD_DIM)
        k = k.reshape(b, s, ATTN_HEADS, HEAD_DIM)
        v = v.reshape(b, s, ATTN_HEADS, HEAD_DIM)
        y_a = chunk_band_attention(q, k, v, rel_bias[l]) @ w_attn_out[l]
        y_b = multiscale_pool(u, w_pool_group[l], pool_scale[l]) @ w_pool_out[l]
        g_a, g_b = jnp.split(jax.nn.sigmoid(gates + b_gate[l]), N_BRANCH, axis=-1)
        mix = (g_a * y_a + g_b * y_b) @ w_o[l]
        x = x + rms_norm(mix, norm_mix_post[l])
        f = conv_gated_ffn(rms_norm(x, norm_ffn_pre[l]), w_up[l], conv_w[l], conv_b[l], w_down[l])
        x = x + rms_norm(f, norm_ffn_post[l])
    return x
```

```python
import functools

import jax
import jax.numpy as jnp
from jax import lax
from jax.experimental import pallas as pl
from jax.experimental.pallas import tpu as pltpu

D_MODEL = 1024
CHUNK = 64
BAND_CHUNKS = 9
ATTN_HEADS = 8
HEAD_DIM = 64
ATTN_WIDTH = ATTN_HEADS * HEAD_DIM
POOL_WINDOWS = (2, 4, 8, 16)
POOL_GROUPS = len(POOL_WINDOWS)
POOL_WIDTH = D_MODEL // 2
POOL_GROUP_DIM = POOL_WIDTH // POOL_GROUPS
MAX_REL_DIST = 256
D_FF = 2816
CONV_WIDTH = 3
EPS = 1e-6

BAND = BAND_CHUNKS * CHUNK
LEAD = (BAND_CHUNKS - 1) * CHUNK
ATTN_TILE = LEAD
CHUNKS_PER_TILE = ATTN_TILE // CHUNK
POOL_HALO = max(POOL_WINDOWS)
LANES = 128
SUBLANES = 8
HEADS_PER_VREG = LANES // HEAD_DIM
FF_CHUNK = 256
VMEM_LIMIT = 56 * 1024 * 1024

TM_PROJ = 512
TM_MIX = 256
TM_FFN = 512

_BF16 = jnp.bfloat16
_F32 = jnp.float32


def _rms(x, g):
    return x * lax.rsqrt(jnp.mean(x * x, axis=-1, keepdims=True) + EPS) * g


def _const_spec(shape):
    zeros = (0,) * len(shape)
    return pl.BlockSpec(shape, lambda *_: zeros)


def _inproj_kernel(x_ref, g_ref, w_ref, q_ref, k_ref, v_ref, u_ref):
    h = _rms(x_ref[...], g_ref[...]).astype(_BF16)
    aw = ATTN_WIDTH
    q = jnp.dot(h, w_ref[:, 0:aw], preferred_element_type=_F32)
    q_ref[...] = (q * (HEAD_DIM ** -0.5)).astype(_BF16)
    k_ref[...] = jnp.dot(h, w_ref[:, aw:2 * aw], preferred_element_type=_F32).astype(_BF16)
    v_ref[...] = jnp.dot(h, w_ref[:, 2 * aw:3 * aw], preferred_element_type=_F32).astype(_BF16)
    u_ref[...] = jnp.dot(h, w_ref[:, 3 * aw:3 * aw + POOL_WIDTH], preferred_element_type=_F32)


def _inproj(x2, g, w_qkvu):
    n = x2.shape[0]
    tm = TM_PROJ
    wcols = w_qkvu.shape[1]
    tok = lambda width: pl.BlockSpec((tm, width), lambda i: (i, 0))
    return pl.pallas_call(
        _inproj_kernel,
        grid=(n // tm,),
        in_specs=[tok(D_MODEL), _const_spec((1, D_MODEL)), _const_spec((D_MODEL, wcols))],
        out_specs=[tok(ATTN_WIDTH), tok(ATTN_WIDTH), tok(ATTN_WIDTH), tok(POOL_WIDTH)],
        out_shape=[jax.ShapeDtypeStruct((n, ATTN_WIDTH), _BF16)] * 3
        + [jax.ShapeDtypeStruct((n, POOL_WIDTH), _F32)],
        compiler_params=pltpu.CompilerParams(
            dimension_semantics=("parallel",), vmem_limit_bytes=VMEM_LIMIT),
        name="inproj",
    )(x2, g, w_qkvu)


def _attn_kernel(q_ref, kc_ref, kp_ref, vc_ref, vp_ref, bias_ref, o_ref, kbuf, vbuf):
    j = pl.program_id(1)
    t = ATTN_TILE
    kbuf[0:t, :] = kp_ref[...]
    kbuf[t:2 * t, :] = kc_ref[...]
    vbuf[0:t, :] = vp_ref[...]
    vbuf[t:2 * t, :] = vc_ref[...]
    lane = lax.broadcasted_iota(jnp.int32, (CHUNK, LANES), 1)
    low_half = lane < HEAD_DIM
    key_pos = lax.broadcasted_iota(jnp.int32, (CHUNK, BAND), 1)

    def chunk_body(c, carry):
        r0 = pl.multiple_of(c * CHUNK, CHUNK)
        first_valid = jnp.where(j == 0, LEAD - c * CHUNK, 0)
        valid = key_pos >= first_valid
        for p in range(ATTN_HEADS // HEADS_PER_VREG):
            cols = slice(p * LANES, (p + 1) * LANES)
            q_pair = q_ref[pl.ds(r0, CHUNK), cols]
            k_pair = kbuf[pl.ds(r0, BAND), cols]
            v_pair = vbuf[pl.ds(r0, BAND), cols]
            outs = []
            for hh in range(HEADS_PER_VREG):
                mask = low_half if hh == 0 else jnp.logical_not(low_half)
                q_h = jnp.where(mask, q_pair, jnp.zeros_like(q_pair))
                s = lax.dot_general(q_h, k_pair, (((1,), (1,)), ((), ())),
                                    preferred_element_type=_F32)
                s = s + bias_ref[p * HEADS_PER_VREG + hh]
                s = jnp.where(valid, s, -1e30)
                m = jnp.max(s, axis=-1, keepdims=True)
                e = jnp.exp(s - m)
                l = jnp.sum(e, axis=-1, keepdims=True)
                o = jnp.dot(e.astype(_BF16), v_pair, preferred_element_type=_F32)
                outs.append(o / l)
            o_ref[pl.ds(r0, CHUNK), cols] = jnp.where(low_half, outs[0], outs[1]).astype(_BF16)
        return carry

    lax.fori_loop(0, CHUNKS_PER_TILE, chunk_body, 0)


def _attention(q, k, v, bias, batch, seq):
    t = ATTN_TILE
    tiles = seq // t
    cur = pl.BlockSpec((t, ATTN_WIDTH), lambda b, j: (b * tiles + j, 0))
    prev = pl.BlockSpec((t, ATTN_WIDTH), lambda b, j: (b * tiles + jnp.maximum(j - 1, 0), 0))
    return pl.pallas_call(
        _attn_kernel,
        grid=(batch, tiles),
        in_specs=[cur, cur, prev, cur, prev, _const_spec((ATTN_HEADS, CHUNK, BAND))],
        out_specs=cur,
        out_shape=jax.ShapeDtypeStruct((batch * seq, ATTN_WIDTH), _BF16),
        scratch_shapes=[pltpu.VMEM((2 * t, ATTN_WIDTH), _BF16),
                        pltpu.VMEM((2 * t, ATTN_WIDTH), _BF16)],
        compiler_params=pltpu.CompilerParams(
            dimension_semantics=("parallel", "parallel"), vmem_limit_bytes=VMEM_LIMIT),
        name="attn",
    )(q, k, k, v, v, bias)


def _mix_kernel(x_ref, attn_ref, u_ref, uh_ref, gpre_ref, wg_ref, bg_ref, wao_ref, wpg_ref,
                pscale_ref, wpo_ref, wo_ref, gpost_ref, out_ref, ubuf):
    j = pl.program_id(1)
    tm = TM_MIX
    x = x_ref[...]

    halo = uh_ref[...]
    ubuf[0:POOL_HALO, :] = jnp.where(j == 0, jnp.zeros_like(halo), halo)
    ubuf[POOL_HALO:, :] = u_ref[...]
    pos = j * tm + lax.broadcasted_iota(jnp.int32, (tm, POOL_GROUP_DIM), 0)
    mixed = []
    for g, w in enumerate(POOL_WINDOWS):
        cols = slice(g * POOL_GROUP_DIM, (g + 1) * POOL_GROUP_DIM)
        ext = ubuf[:, cols]
        win = ext
        span = 1
        while span < w:
            win = win + pltpu.roll(win, span, 0)
            span *= 2
        cnt = jnp.minimum(pos + 1, w).astype(_F32)
        pooled = win[POOL_HALO:, :] / cnt - ext[POOL_HALO:, :]
        m = jnp.dot(pooled.astype(_BF16), wpg_ref[g], preferred_element_type=_F32)
        mixed.append((m * pscale_ref[:, cols]).astype(_BF16))
    y_b = jnp.dot(jnp.concatenate(mixed, axis=-1), wpo_ref[...], preferred_element_type=_F32)
    y_a = jnp.dot(attn_ref[...], wao_ref[...], preferred_element_type=_F32)

    h = _rms(x, gpre_ref[...]).astype(_BF16)
    gates = jax.nn.sigmoid(jnp.dot(h, wg_ref[...], preferred_element_type=_F32) + bg_ref[...])
    mix = gates[:, :D_MODEL] * y_a + gates[:, D_MODEL:] * y_b
    mo = jnp.dot(mix.astype(_BF16), wo_ref[...], preferred_element_type=_F32)
    out_ref[...] = x + _rms(mo, gpost_ref[...])


def _mix(x2, attn, u, gpre, wg, bg, wao, wpg, pscale, wpo, wo, gpost, batch, seq):
    tm = TM_MIX
    tiles = seq // tm
    halo_per_tile = tm // POOL_HALO
    tok = lambda width: pl.BlockSpec((tm, width), lambda b, j: (b * tiles + j, 0))
    halo = pl.BlockSpec(
        (POOL_HALO, POOL_WIDTH),
        lambda b, j: (jnp.maximum((b * tiles + j) * halo_per_tile - 1, 0), 0))
    return pl.pallas_call(
        _mix_kernel,
        grid=(batch, tiles),
        in_specs=[tok(D_MODEL), tok(ATTN_WIDTH), tok(POOL_WIDTH), halo,
                  _const_spec((1, D_MODEL)), _const_spec((D_MODEL, 2 * D_MODEL)),
                  _const_spec((1, 2 * D_MODEL)), _const_spec((ATTN_WIDTH, D_MODEL)),
                  _const_spec((POOL_GROUPS, POOL_GROUP_DIM, POOL_GROUP_DIM)),
                  _const_spec((1, POOL_WIDTH)), _const_spec((POOL_WIDTH, D_MODEL)),
                  _const_spec((D_MODEL, D_MODEL)), _const_spec((1, D_MODEL))],
        out_specs=tok(D_MODEL),
        out_shape=jax.ShapeDtypeStruct((batch * seq, D_MODEL), _F32),
        scratch_shapes=[pltpu.VMEM((tm + POOL_HALO, POOL_WIDTH), _F32)],
        compiler_params=pltpu.CompilerParams(
            dimension_semantics=("parallel", "parallel"), vmem_limit_bytes=VMEM_LIMIT),
        name="mix",
    )(x2, attn, u, u, gpre, wg, bg, wao, wpg, pscale, wpo, wo, gpost)


def _gelu_tanh(x):
    return 0.5 * x * (1.0 + jnp.tanh(0.7978845608028654 * (x + 0.044715 * (x * x * x))))


def _ffn_kernel(x_ref, gpre_ref, wup_ref, cw_ref, cb_ref, wdn_ref, gpost_ref, out_ref,
                carry_ref, acc_ref):
    j = pl.program_id(1)
    tm = TM_FFN

    @pl.when(j == 0)
    def _():
        carry_ref[...] = jnp.zeros_like(carry_ref)

    x = x_ref[...]
    h = _rms(x, gpre_ref[...]).astype(_BF16)
    top_row = lax.broadcasted_iota(jnp.int32, (SUBLANES, FF_CHUNK), 0)

    def conv_branch(col):
        cs = slice(col, col + FF_CHUNK)
        hu = jnp.dot(h, wup_ref[:, cs], preferred_element_type=_F32)
        prev = carry_ref[:, cs]
        carry_ref[:, cs] = hu[tm - SUBLANES:, :]
        p1 = prev[SUBLANES - 1:SUBLANES, :]
        p2 = prev[SUBLANES - 2:SUBLANES - 1, :]
        r1 = pltpu.roll(hu, 1, 0)
        r2 = pltpu.roll(hu, 2, 0)
        r1_top = jnp.where(top_row == 0, p1, r1[:SUBLANES])
        r2_top = jnp.where(top_row == 0, p2, jnp.where(top_row == 1, p1, r2[:SUBLANES]))
        hm1 = jnp.concatenate([r1_top, r1[SUBLANES:]], axis=0)
        hm2 = jnp.concatenate([r2_top, r2[SUBLANES:]], axis=0)
        return (cb_ref[:, cs] + cw_ref[2:3, cs] * hu) + cw_ref[0:1, cs] * hm2 + cw_ref[1:2, cs] * hm1

    for c in range(D_FF // FF_CHUNK):
        val = conv_branch(c * FF_CHUNK)
        gate = conv_branch(D_FF + c * FF_CHUNK)
        act = (_gelu_tanh(gate) * val).astype(_BF16)
        contrib = jnp.dot(act, wdn_ref[c * FF_CHUNK:(c + 1) * FF_CHUNK, :],
                          preferred_element_type=_F32)
        if c == 0:
            acc_ref[...] = contrib
        else:
            acc_ref[...] += contrib
    out_ref[...] = x + _rms(acc_ref[...], gpost_ref[...])


def _ffn(x2, gpre, wup, cw, cb, wdn, gpost, batch, seq):
    tm = TM_FFN
    tiles = seq // tm
    tok = pl.BlockSpec((tm, D_MODEL), lambda b, j: (b * tiles + j, 0))
    return pl.pallas_call(
        _ffn_kernel,
        grid=(batch, tiles),
        in_specs=[tok, _const_spec((1, D_MODEL)), _const_spec((D_MODEL, 2 * D_FF)),
                  _const_spec((CONV_WIDTH, 2 * D_FF)), _const_spec((1, 2 * D_FF)),
                  _const_spec((D_FF, D_MODEL)), _const_spec((1, D_MODEL))],
        out_specs=tok,
        out_shape=jax.ShapeDtypeStruct((batch * seq, D_MODEL), _F32),
        scratch_shapes=[pltpu.VMEM((SUBLANES, 2 * D_FF), _F32),
                        pltpu.VMEM((tm, D_MODEL), _F32)],
        compiler_params=pltpu.CompilerParams(
            dimension_semantics=("arbitrary", "arbitrary"), vmem_limit_bytes=VMEM_LIMIT),
        name="ffn",
    )(x2, gpre, wup, cw, cb, wdn, gpost)


def _band_bias(rel_bias):
    dist = jnp.arange(CHUNK)[:, None] + LEAD - jnp.arange(BAND)[None, :]
    idx = jnp.clip(dist, -MAX_REL_DIST, MAX_REL_DIST) + MAX_REL_DIST
    return rel_bias.astype(_F32)[:, idx]


def kernel(x, norm_mix_pre, w_in, b_gate, rel_bias, w_attn_out, w_pool_group, pool_scale,
           w_pool_out, w_o, norm_mix_post, norm_ffn_pre, w_up, conv_w, conv_b, w_down,
           norm_ffn_post):
    batch, seq, d = x.shape
    assert d == D_MODEL and seq % ATTN_TILE == 0 and seq % TM_FFN == 0 and seq % TM_MIX == 0
    depth = w_in.shape[0]
    n_qkvu = 3 * ATTN_WIDTH + POOL_WIDTH
    x2 = x.reshape(batch * seq, d)
    row = lambda a: a.reshape(1, -1)
    for l in range(depth):
        w_in_l = w_in[l].astype(_BF16)
        q, k, v, u = _inproj(x2, row(norm_mix_pre[l]), w_in_l[:, :n_qkvu])
        attn = _attention(q, k, v, _band_bias(rel_bias[l]), batch, seq)
        x2 = _mix(x2, attn, u, row(norm_mix_pre[l]), w_in_l[:, n_qkvu:], row(b_gate[l]),
                  w_attn_out[l].astype(_BF16), w_pool_group[l].astype(_BF16), row(pool_scale[l]),
                  w_pool_out[l].astype(_BF16), w_o[l].astype(_BF16), row(norm_mix_post[l]),
                  batch, seq)
        x2 = _ffn(x2, row(norm_ffn_pre[l]), w_up[l].astype(_BF16), conv_w[l], row(conv_b[l]),
                  w_down[l].astype(_BF16), row(norm_ffn_post[l]), batch, seq)
    return x2.reshape(batch, seq, d)
```

```python
import functools

import jax
import jax.numpy as jnp
from jax import lax
from jax.experimental import pallas as pl
from jax.experimental.pallas import tpu as pltpu

D_MODEL = 1024
CHUNK = 64
BAND_CHUNKS = 9
ATTN_HEADS = 8
HEAD_DIM = 64
ATTN_WIDTH = ATTN_HEADS * HEAD_DIM
POOL_WINDOWS = (2, 4, 8, 16)
POOL_GROUPS = len(POOL_WINDOWS)
POOL_WIDTH = D_MODEL // 2
POOL_GROUP_DIM = POOL_WIDTH // POOL_GROUPS
MAX_REL_DIST = 256
D_FF = 2816
CONV_WIDTH = 3
EPS = 1e-6

BAND = BAND_CHUNKS * CHUNK
LEAD = (BAND_CHUNKS - 1) * CHUNK
ATTN_TILE = LEAD
CHUNKS_PER_TILE = ATTN_TILE // CHUNK
POOL_HALO = max(POOL_WINDOWS)
LANES = 128
SUBLANES = 8
HEADS_PER_VREG = LANES // HEAD_DIM
FF_CHUNK = 256
VMEM_LIMIT = 56 * 1024 * 1024

TM_PROJ = 512
TM_MIX = 256
TM_FFN = 512

_BF16 = jnp.bfloat16
_F32 = jnp.float32


def _rms(x, g):
    return x * lax.rsqrt(jnp.mean(x * x, axis=-1, keepdims=True) + EPS) * g


def _const_spec(shape):
    zeros = (0,) * len(shape)
    return pl.BlockSpec(shape, lambda *_: zeros)


def _inproj_kernel(x_ref, g_ref, w_ref, q_ref, k_ref, v_ref, u_ref):
    h = _rms(x_ref[...], g_ref[...]).astype(_BF16)
    aw = ATTN_WIDTH
    q = jnp.dot(h, w_ref[:, 0:aw], preferred_element_type=_F32)
    q_ref[...] = (q * (HEAD_DIM ** -0.5)).astype(_BF16)
    k_ref[...] = jnp.dot(h, w_ref[:, aw:2 * aw], preferred_element_type=_F32).astype(_BF16)
    v_ref[...] = jnp.dot(h, w_ref[:, 2 * aw:3 * aw], preferred_element_type=_F32).astype(_BF16)
    u_ref[...] = jnp.dot(h, w_ref[:, 3 * aw:3 * aw + POOL_WIDTH], preferred_element_type=_F32)


def _inproj(x2, g, w_qkvu):
    n = x2.shape[0]
    tm = TM_PROJ
    wcols = w_qkvu.shape[1]
    tok = lambda width: pl.BlockSpec((tm, width), lambda i: (i, 0))
    return pl.pallas_call(
        _inproj_kernel,
        grid=(n // tm,),
        in_specs=[tok(D_MODEL), _const_spec((1, D_MODEL)), _const_spec((D_MODEL, wcols))],
        out_specs=[tok(ATTN_WIDTH), tok(ATTN_WIDTH), tok(ATTN_WIDTH), tok(POOL_WIDTH)],
        out_shape=[jax.ShapeDtypeStruct((n, ATTN_WIDTH), _BF16)] * 3
        + [jax.ShapeDtypeStruct((n, POOL_WIDTH), _F32)],
        compiler_params=pltpu.CompilerParams(
            dimension_semantics=("parallel",), vmem_limit_bytes=VMEM_LIMIT),
        name="inproj",
    )(x2, g, w_qkvu)


def _attn_kernel(q_ref, kc_ref, kp_ref, vc_ref, vp_ref, bias_ref, o_ref, kbuf, vbuf):
    j = pl.program_id(1)
    t = ATTN_TILE
    kbuf[0:t, :] = kp_ref[...]
    kbuf[t:2 * t, :] = kc_ref[...]
    vbuf[0:t, :] = vp_ref[...]
    vbuf[t:2 * t, :] = vc_ref[...]
    lane = lax.broadcasted_iota(jnp.int32, (CHUNK, LANES), 1)
    low_half = lane < HEAD_DIM
    pairs = ATTN_HEADS // HEADS_PER_VREG

    def chunk_body(c, carry, *, first_tile):
        r0 = pl.multiple_of(c * CHUNK, CHUNK)
        scores = []
        for p in range(pairs):
            cols = slice(p * LANES, (p + 1) * LANES)
            q_pair = q_ref[pl.ds(r0, CHUNK), cols]
            k_pair = kbuf[pl.ds(r0, BAND), cols]
            for hh in range(HEADS_PER_VREG):
                mask = low_half if hh == 0 else jnp.logical_not(low_half)
                q_h = jnp.where(mask, q_pair, jnp.zeros_like(q_pair))
                scores.append(lax.dot_general(q_h, k_pair, (((1,), (1,)), ((), ())),
                                              preferred_element_type=_F32))
        if first_tile:
            key_pos = lax.broadcasted_iota(jnp.int32, (CHUNK, BAND), 1)
            valid = key_pos >= LEAD - c * CHUNK
        probs, denoms = [], []
        for h in range(ATTN_HEADS):
            s = scores[h] + bias_ref[h]
            if first_tile:
                s = jnp.where(valid, s, -1e30)
            m = jnp.max(s, axis=-1, keepdims=True)
            e = jnp.exp(s - m)
            denoms.append(jnp.sum(e, axis=-1, keepdims=True))
            probs.append(e.astype(_BF16))
        for p in range(pairs):
            cols = slice(p * LANES, (p + 1) * LANES)
            v_pair = vbuf[pl.ds(r0, BAND), cols]
            outs = []
            for hh in range(HEADS_PER_VREG):
                h = p * HEADS_PER_VREG + hh
                o = jnp.dot(probs[h], v_pair, preferred_element_type=_F32)
                outs.append(o / denoms[h])
            o_ref[pl.ds(r0, CHUNK), cols] = jnp.where(low_half, outs[0], outs[1]).astype(_BF16)
        return carry

    @pl.when(j == 0)
    def _():
        lax.fori_loop(0, CHUNKS_PER_TILE, functools.partial(chunk_body, first_tile=True), 0)

    @pl.when(j != 0)
    def _():
        lax.fori_loop(0, CHUNKS_PER_TILE, functools.partial(chunk_body, first_tile=False), 0)


def _attention(q, k, v, bias, batch, seq):
    t = ATTN_TILE
    tiles = seq // t
    cur = pl.BlockSpec((t, ATTN_WIDTH), lambda b, j: (b * tiles + j, 0))
    prev = pl.BlockSpec((t, ATTN_WIDTH), lambda b, j: (b * tiles + jnp.maximum(j - 1, 0), 0))
    return pl.pallas_call(
        _attn_kernel,
        grid=(batch, tiles),
        in_specs=[cur, cur, prev, cur, prev, _const_spec((ATTN_HEADS, CHUNK, BAND))],
        out_specs=cur,
        out_shape=jax.ShapeDtypeStruct((batch * seq, ATTN_WIDTH), _BF16),
        scratch_shapes=[pltpu.VMEM((2 * t, ATTN_WIDTH), _BF16),
                        pltpu.VMEM((2 * t, ATTN_WIDTH), _BF16)],
        compiler_params=pltpu.CompilerParams(
            dimension_semantics=("parallel", "parallel"), vmem_limit_bytes=VMEM_LIMIT),
        name="attn",
    )(q, k, k, v, v, bias)


def _mix_kernel(x_ref, attn_ref, u_ref, uh_ref, gpre_ref, wg_ref, bg_ref, wao_ref, wpg_ref,
                pscale_ref, wpo_ref, wo_ref, gpost_ref, out_ref, ubuf):
    j = pl.program_id(1)
    tm = TM_MIX
    x = x_ref[...]

    h = _rms(x, gpre_ref[...]).astype(_BF16)
    gates_pre = jnp.dot(h, wg_ref[...], preferred_element_type=_F32)
    y_a = jnp.dot(attn_ref[...], wao_ref[...], preferred_element_type=_F32)

    halo = uh_ref[...]
    ubuf[0:POOL_HALO, :] = jnp.where(j == 0, jnp.zeros_like(halo), halo)
    ubuf[POOL_HALO:, :] = u_ref[...]
    pos = j * tm + lax.broadcasted_iota(jnp.int32, (tm, POOL_GROUP_DIM), 0)
    mixed = []
    for g, w in enumerate(POOL_WINDOWS):
        cols = slice(g * POOL_GROUP_DIM, (g + 1) * POOL_GROUP_DIM)
        ext = ubuf[:, cols]
        win = ext
        span = 1
        while span < w:
            win = win + pltpu.roll(win, span, 0)
            span *= 2
        cnt = jnp.minimum(pos + 1, w).astype(_F32)
        pooled = win[POOL_HALO:, :] / cnt - ext[POOL_HALO:, :]
        m = jnp.dot(pooled.astype(_BF16), wpg_ref[g], preferred_element_type=_F32)
        mixed.append((m * pscale_ref[:, cols]).astype(_BF16))
    y_b = jnp.dot(jnp.concatenate(mixed, axis=-1), wpo_ref[...], preferred_element_type=_F32)

    gates = jax.nn.sigmoid(gates_pre + bg_ref[...])
    mix = gates[:, :D_MODEL] * y_a + gates[:, D_MODEL:] * y_b
    mo = jnp.dot(mix.astype(_BF16), wo_ref[...], preferred_element_type=_F32)
    out_ref[...] = x + _rms(mo, gpost_ref[...])


def _mix(x2, attn, u, gpre, wg, bg, wao, wpg, pscale, wpo, wo, gpost, batch, seq):
    tm = TM_MIX
    tiles = seq // tm
    halo_per_tile = tm // POOL_HALO
    tok = lambda width: pl.BlockSpec((tm, width), lambda b, j: (b * tiles + j, 0))
    halo = pl.BlockSpec(
        (POOL_HALO, POOL_WIDTH),
        lambda b, j: (jnp.maximum((b * tiles + j) * halo_per_tile - 1, 0), 0))
    return pl.pallas_call(
        _mix_kernel,
        grid=(batch, tiles),
        in_specs=[tok(D_MODEL), tok(ATTN_WIDTH), tok(POOL_WIDTH), halo,
                  _const_spec((1, D_MODEL)), _const_spec((D_MODEL, 2 * D_MODEL)),
                  _const_spec((1, 2 * D_MODEL)), _const_spec((ATTN_WIDTH, D_MODEL)),
                  _const_spec((POOL_GROUPS, POOL_GROUP_DIM, POOL_GROUP_DIM)),
                  _const_spec((1, POOL_WIDTH)), _const_spec((POOL_WIDTH, D_MODEL)),
                  _const_spec((D_MODEL, D_MODEL)), _const_spec((1, D_MODEL))],
        out_specs=tok(D_MODEL),
        out_shape=jax.ShapeDtypeStruct((batch * seq, D_MODEL), _F32),
        scratch_shapes=[pltpu.VMEM((tm + POOL_HALO, POOL_WIDTH), _F32)],
        compiler_params=pltpu.CompilerParams(
            dimension_semantics=("parallel", "parallel"), vmem_limit_bytes=VMEM_LIMIT),
        name="mix",
    )(x2, attn, u, u, gpre, wg, bg, wao, wpg, pscale, wpo, wo, gpost)


def _gelu_tanh(x):
    return 0.5 * x * (1.0 + jnp.tanh(0.7978845608028654 * (x + 0.044715 * (x * x * x))))


def _ffn_kernel(x_ref, gpre_ref, wup_ref, cw_ref, cb_ref, wdn_ref, gpost_ref, out_ref,
                carry_ref, acc_ref):
    j = pl.program_id(1)
    tm = TM_FFN

    @pl.when(j == 0)
    def _():
        carry_ref[...] = jnp.zeros_like(carry_ref)

    x = x_ref[...]
    h = _rms(x, gpre_ref[...]).astype(_BF16)

    top_row = lax.broadcasted_iota(jnp.int32, (SUBLANES, FF_CHUNK), 0)

    def up_proj(c):
        return [jnp.dot(h, wup_ref[:, col:col + FF_CHUNK], preferred_element_type=_F32)
                for col in (c * FF_CHUNK, D_FF + c * FF_CHUNK)]

    def conv_branch(hu, col):
        cs = slice(col, col + FF_CHUNK)
        prev = carry_ref[:, cs]
        carry_ref[:, cs] = hu[tm - SUBLANES:, :]
        p1 = prev[SUBLANES - 1:SUBLANES, :]
        p2 = prev[SUBLANES - 2:SUBLANES - 1, :]
        r1 = pltpu.roll(hu, 1, 0)
        r2 = pltpu.roll(hu, 2, 0)
        r1_top = jnp.where(top_row == 0, p1, r1[:SUBLANES])
        r2_top = jnp.where(top_row == 0, p2, jnp.where(top_row == 1, p1, r2[:SUBLANES]))
        hm1 = jnp.concatenate([r1_top, r1[SUBLANES:]], axis=0)
        hm2 = jnp.concatenate([r2_top, r2[SUBLANES:]], axis=0)
        return (cb_ref[:, cs] + cw_ref[2:3, cs] * hu) + cw_ref[0:1, cs] * hm2 + cw_ref[1:2, cs] * hm1

    n_chunks = D_FF // FF_CHUNK
    hu_next = up_proj(0)
    for c in range(n_chunks):
        hu_val, hu_gate = hu_next
        if c + 1 < n_chunks:
            hu_next = up_proj(c + 1)
        val = conv_branch(hu_val, c * FF_CHUNK)
        gate = conv_branch(hu_gate, D_FF + c * FF_CHUNK)
        act = (_gelu_tanh(gate) * val).astype(_BF16)
        contrib = jnp.dot(act, wdn_ref[c * FF_CHUNK:(c + 1) * FF_CHUNK, :],
                          preferred_element_type=_F32)
        if c == 0:
            acc_ref[...] = contrib
        else:
            acc_ref[...] += contrib
    out_ref[...] = x + _rms(acc_ref[...], gpost_ref[...])


def _ffn(x2, gpre, wup, cw, cb, wdn, gpost, batch, seq):
    tm = TM_FFN
    tiles = seq // tm
    tok = pl.BlockSpec((tm, D_MODEL), lambda b, j: (b * tiles + j, 0))
    return pl.pallas_call(
        _ffn_kernel,
        grid=(batch, tiles),
        in_specs=[tok, _const_spec((1, D_MODEL)), _const_spec((D_MODEL, 2 * D_FF)),
                  _const_spec((CONV_WIDTH, 2 * D_FF)), _const_spec((1, 2 * D_FF)),
                  _const_spec((D_FF, D_MODEL)), _const_spec((1, D_MODEL))],
        out_specs=tok,
        out_shape=jax.ShapeDtypeStruct((batch * seq, D_MODEL), _F32),
        scratch_shapes=[pltpu.VMEM((SUBLANES, 2 * D_FF), _F32),
                        pltpu.VMEM((tm, D_MODEL), _F32)],
        compiler_params=pltpu.CompilerParams(
            dimension_semantics=("arbitrary", "arbitrary"), vmem_limit_bytes=VMEM_LIMIT),
        name="ffn",
    )(x2, gpre, wup, cw, cb, wdn, gpost)


def _band_bias(rel_bias):
    dist = jnp.arange(CHUNK - 1 + LEAD, -BAND, -1)
    table = rel_bias.astype(_F32)[:, jnp.clip(dist, -MAX_REL_DIST, MAX_REL_DIST) + MAX_REL_DIST]
    rows = [table[:, CHUNK - 1 - i:CHUNK - 1 - i + BAND] for i in range(CHUNK)]
    return jnp.stack(rows, axis=1)


def kernel(x, norm_mix_pre, w_in, b_gate, rel_bias, w_attn_out, w_pool_group, pool_scale,
           w_pool_out, w_o, norm_mix_post, norm_ffn_pre, w_up, conv_w, conv_b, w_down,
           norm_ffn_post):
    batch, seq, d = x.shape
    assert d == D_MODEL and seq % ATTN_TILE == 0 and seq % TM_FFN == 0 and seq % TM_MIX == 0
    depth = w_in.shape[0]
    n_qkvu = 3 * ATTN_WIDTH + POOL_WIDTH
    x2 = x.reshape(batch * seq, d)
    row = lambda a: a.reshape(1, -1)
    for l in range(depth):
        w_in_l = w_in[l].astype(_BF16)
        q, k, v, u = _inproj(x2, row(norm_mix_pre[l]), w_in_l[:, :n_qkvu])
        attn = _attention(q, k, v, _band_bias(rel_bias[l]), batch, seq)
        x2 = _mix(x2, attn, u, row(norm_mix_pre[l]), w_in_l[:, n_qkvu:], row(b_gate[l]),
                  w_attn_out[l].astype(_BF16), w_pool_group[l].astype(_BF16), row(pool_scale[l]),
                  w_pool_out[l].astype(_BF16), w_o[l].astype(_BF16), row(norm_mix_post[l]),
                  batch, seq)
        x2 = _ffn(x2, row(norm_ffn_pre[l]), w_up[l].astype(_BF16), conv_w[l], row(conv_b[l]),
                  w_down[l].astype(_BF16), row(norm_ffn_post[l]), batch, seq)
    return x2.reshape(batch, seq, d)
```
